```python
import math
import jax, jax.numpy as jnp
from jax import lax
import numpy as np

D_MODEL = 2048
BATCH = 4
SEQ = 2048
DEPTH = 4
DEC_BATCH = 32
DEC_SEQ = 1
PAST_LEN = 16384
PAGE_SIZE = 128

D_MIX = D_MODEL
D_POOL = D_MIX // 4
POOL_WINDOWS = (2, 4, 8, 16)
N_POOL_GROUPS = len(POOL_WINDOWS)
POOL_GROUP = D_POOL // N_POOL_GROUPS
POOL_STATE = max(POOL_WINDOWS) - 1
D_CONV = D_MIX // 4
CONV_WIDTH = 31
CONV_STATE = CONV_WIDTH - 1
D_ATTN = D_MIX // 2
HEAD_DIM = 64
N_HEADS = D_ATTN // HEAD_DIM
N_KV_HEADS = 4
GQA_GROUP = N_HEADS // N_KV_HEADS
D_KV = N_KV_HEADS * HEAD_DIM
WINDOW = 128
BLOCK = WINDOW
ATTN_SCALE = HEAD_DIM ** -0.5
NEG_INF = -1e30
N_BUCKETS = 32
MAX_EXACT = N_BUCKETS // 2
MAX_DISTANCE = 128
D_IN = D_POOL + 2 * D_CONV + D_ATTN + 2 * D_KV
SPLITS = (D_POOL, D_POOL + D_CONV, D_POOL + 2 * D_CONV,
          D_POOL + 2 * D_CONV + D_ATTN, D_POOL + 2 * D_CONV + D_ATTN + D_KV)
D_FF = 5632
D_PLE = 256
LN_EPS = 1e-5
ALPHA = (2 * DEPTH) ** 0.25
BETA = (8 * DEPTH) ** -0.25

kernel_name = 'hybrid_pool_conv_swa_macaron_deepnorm_step'


def _layer_norm(x, g, b):
    xf = x.astype(jnp.float32)
    mu = jnp.mean(xf, axis=-1, keepdims=True)
    var = jnp.mean(jnp.square(xf - mu), axis=-1, keepdims=True)
    y = (xf - mu) * lax.rsqrt(var + LN_EPS) * g.astype(jnp.float32) + b.astype(jnp.float32)
    return y.astype(x.dtype)


def _swiglu(x, w_gate, w_up, w_down):
    return (jax.nn.silu(x @ w_gate) * (x @ w_up)) @ w_down


def _t5_bucket(dist):
    n = jnp.maximum(dist, 0)
    nf = jnp.maximum(n, 1).astype(jnp.float32)
    log_b = MAX_EXACT + (jnp.log(nf / MAX_EXACT) / math.log(MAX_DISTANCE / MAX_EXACT)
                         * (N_BUCKETS - MAX_EXACT)).astype(jnp.int32)
    return jnp.where(n < MAX_EXACT, n, jnp.minimum(log_b, N_BUCKETS - 1))


def _multiscale_pool(u_ext, positions, w_pool, pool_scale):
    L = positions.shape[0]
    uf = u_ext.astype(jnp.float32)
    c = jnp.pad(jnp.cumsum(uf, axis=1), ((0, 0), (1, 0), (0, 0)))
    cur = uf[:, POOL_STATE:]
    diffs = []
    for g, w in enumerate(POOL_WINDOWS):
        sl = slice(g * POOL_GROUP, (g + 1) * POOL_GROUP)
        win_sum = (c[:, POOL_STATE + 1:POOL_STATE + 1 + L, sl]
                   - c[:, POOL_STATE + 1 - w:POOL_STATE + 1 - w + L, sl])
        count = jnp.minimum(positions + 1, w).astype(jnp.float32)[None, :, None]
        diffs.append(win_sum / count - cur[..., sl])
    d = jnp.stack(diffs, axis=2).astype(u_ext.dtype)
    y = jnp.einsum('blgc,gcd->blgd', d, w_pool)
    return y.reshape(y.shape[0], L, D_POOL) * pool_scale


def _conformer_conv(v_ext, w_dw, b_dw, ln_g, ln_b, w_pw):
    y = lax.conv_general_dilated(v_ext, w_dw[:, None, :], window_strides=(1,), padding='VALID',
                                 dimension_numbers=('NWC', 'WIO', 'NWC'),
                                 feature_group_count=D_CONV)
    y = jax.nn.silu(_layer_norm(y + b_dw, ln_g, ln_b))
    return y @ w_pw


def _window_softmax(q, k, v, q_pos, k_pos, rel_bias, sinks):
    B, NB, Q = q.shape[:3]
    K = k.shape[2]
    qg = q.reshape(B, NB, Q, N_KV_HEADS, GQA_GROUP, HEAD_DIM)
    s = jnp.einsum('bnqhgd,bnshd->bnhgqs', qg, k,
                   preferred_element_type=jnp.float32) * ATTN_SCALE
    dist = q_pos[:, :, None] - k_pos[:, None, :]
    valid = (dist >= 0) & (dist < WINDOW) & (k_pos[:, None, :] >= 0)
    bias = rel_bias[_t5_bucket(dist)].astype(jnp.float32)
    bias = jnp.moveaxis(bias, -1, 1).reshape(NB, N_KV_HEADS, GQA_GROUP, Q, K)
    s = jnp.where(valid[None, :, None, None], s + bias[None], NEG_INF)
    sink = sinks.astype(jnp.float32).reshape(1, 1, N_KV_HEADS, GQA_GROUP, 1, 1)
    m = jnp.maximum(jnp.max(s, axis=-1, keepdims=True), sink)
    e = jnp.exp(s - m)
    prob = e / (jnp.sum(e, axis=-1, keepdims=True) + jnp.exp(sink - m))
    o = jnp.einsum('bnhgqs,bnshd->bnqhgd', prob.astype(v.dtype), v)
    return o.reshape(B, NB, Q, D_ATTN)


def _token_mixer(h, pos0, pool_past, conv_past, k_past, v_past, prm, l):
    B, L, _ = h.shape
    z = h @ prm['w_in'][l]
    u, a, gt, q, k, v = jnp.split(z, SPLITS, axis=-1)
    positions = pos0 + jnp.arange(L, dtype=jnp.int32)
    u_ext = jnp.concatenate([pool_past, u], axis=1)
    y_pool = _multiscale_pool(u_ext, positions, prm['w_pool'][l], prm['pool_scale'][l])
    c_ext = jnp.concatenate([conv_past, a * jax.nn.sigmoid(gt)], axis=1)
    y_conv = _conformer_conv(c_ext, prm['w_dw'][l], prm['b_dw'][l], prm['conv_ln_g'][l],
                             prm['conv_ln_b'][l], prm['w_pw'][l])
    q = q.reshape(B, L, N_HEADS, HEAD_DIM)
    k = k.reshape(B, L, N_KV_HEADS, HEAD_DIM)
    v = v.reshape(B, L, N_KV_HEADS, HEAD_DIM)
    if k_past is None:
        nb = L // BLOCK
        qb = q.reshape(B, nb, BLOCK, N_HEADS, HEAD_DIM)
        kb = k.reshape(B, nb, BLOCK, N_KV_HEADS, HEAD_DIM)
        vb = v.reshape(B, nb, BLOCK, N_KV_HEADS, HEAD_DIM)
        prev = lambda t: jnp.pad(t, ((0, 0), (1, 0), (0, 0), (0, 0), (0, 0)))[:, :-1]
        k_band = jnp.concatenate([prev(kb), kb], axis=2)
        v_band = jnp.concatenate([prev(vb), vb], axis=2)
        q_pos = positions.reshape(nb, BLOCK)
        k_pos = (pos0 + (jnp.arange(nb, dtype=jnp.int32)[:, None] - 1) * BLOCK
                 + jnp.arange(2 * BLOCK, dtype=jnp.int32)[None])
        y_attn = _window_softmax(qb, k_band, v_band, q_pos, k_pos, prm['rel_bias'],
                                 prm['sinks'][l]).reshape(B, L, D_ATTN)
        k_keep, v_keep = k[:, -WINDOW:], v[:, -WINDOW:]
    else:
        k_ext = jnp.concatenate([k_past, k], axis=1)
        v_ext = jnp.concatenate([v_past, v], axis=1)
        q_pos = positions[None]
        k_pos = (pos0 - WINDOW + jnp.arange(WINDOW + L, dtype=jnp.int32))[None]
        y_attn = _window_softmax(q[:, None], k_ext[:, None], v_ext[:, None], q_pos, k_pos,
                                 prm['rel_bias'], prm['sinks'][l]).reshape(B, L, D_ATTN)
        k_keep, v_keep = k_ext[:, -WINDOW:], v_ext[:, -WINDOW:]
    y = jnp.concatenate([y_pool, y_conv, y_attn], axis=-1) @ prm['w_out'][l]
    return y, (k_keep, v_keep, u_ext[:, -POOL_STATE:], c_ext[:, -CONV_STATE:])


def _trunk(x, p, pos0, cache_k, cache_v, state_pool, state_conv, prm):
    prompt = cache_k is None
    B = x.shape[0]
    h = x
    new = ([], [], [], [])
    for l in range(DEPTH):
        h = _layer_norm(ALPHA * h + 0.5 * _swiglu(h, prm['ffn1_w_gate'][l], prm['ffn1_w_up'][l],
                                                  prm['ffn1_w_down'][l]),
                        prm['ln1_g'][l], prm['ln1_b'][l])
        if prompt:
            pool_past = jnp.zeros((B, POOL_STATE, D_POOL), h.dtype)
            conv_past = jnp.zeros((B, CONV_STATE, D_CONV), h.dtype)
            k_past, v_past = None, None
        else:
            pool_past, conv_past = state_pool[l], state_conv[l]
            k_past, v_past = cache_k[l], cache_v[l]
        y, st = _token_mixer(h, pos0, pool_past, conv_past, k_past, v_past, prm, l)
        h = _layer_norm(ALPHA * h + y, prm['ln2_g'][l], prm['ln2_b'][l])
        h = _layer_norm(ALPHA * h + 0.5 * _swiglu(h, prm['ffn2_w_gate'][l], prm['ffn2_w_up'][l],
                                                  prm['ffn2_w_down'][l]),
                        prm['ln3_g'][l], prm['ln3_b'][l])
        h = h + jax.nn.sigmoid(h @ prm['w_ple_gate'][l]) * (p[l] @ prm['w_ple'][l])
        for lst, s in zip(new, st):
            lst.append(s)
    return h, [jnp.stack(s_list) for s_list in new]


def setup_inputs(seed: int = 0) -> dict:
    key = jax.random.key(seed)
    ks = iter(jax.random.split(key, 48))
    nrm = lambda shape, scale: jax.random.normal(next(ks), shape, jnp.float32) * scale
    gain = lambda shape: 1.0 + nrm(shape, 0.02)
    return {
        'x_prompt': nrm((BATCH, SEQ, D_MODEL), 1.0),
        'x_sample': nrm((DEC_BATCH, DEC_SEQ, D_MODEL), 1.0),
        'p_prompt': nrm((DEPTH, BATCH, SEQ, D_PLE), 1.0),
        'p_sample': nrm((DEPTH, DEC_BATCH, DEC_SEQ, D_PLE), 1.0),
        'cache_k': nrm((DEPTH, DEC_BATCH, WINDOW, N_KV_HEADS, HEAD_DIM), 1.0),
        'cache_v': nrm((DEPTH, DEC_BATCH, WINDOW, N_KV_HEADS, HEAD_DIM), 1.0),
        'state_pool': nrm((DEPTH, DEC_BATCH, POOL_STATE, D_POOL), 1.0),
        'state_conv': nrm((DEPTH, DEC_BATCH, CONV_STATE, D_CONV), 0.5),
        'rel_bias': nrm((N_BUCKETS, N_HEADS), 0.5),
        'ln1_g': gain((DEPTH, D_MODEL)),
        'ln1_b': nrm((DEPTH, D_MODEL), 0.02),
        'ffn1_w_gate': nrm((DEPTH, D_MODEL, D_FF), D_MODEL ** -0.5),
        'ffn1_w_up': nrm((DEPTH, D_MODEL, D_FF), D_MODEL ** -0.5),
        'ffn1_w_down': nrm((DEPTH, D_FF, D_MODEL), BETA * D_FF ** -0.5),
        'w_in': nrm((DEPTH, D_MODEL, D_IN), D_MODEL ** -0.5),
        'w_pool': nrm((DEPTH, N_POOL_GROUPS, POOL_GROUP, POOL_GROUP), POOL_GROUP ** -0.5),
        'pool_scale': gain((DEPTH, D_POOL)),
        'w_dw': nrm((DEPTH, CONV_WIDTH, D_CONV), CONV_WIDTH ** -0.5),
        'b_dw': nrm((DEPTH, D_CONV), 0.01),
        'conv_ln_g': gain((DEPTH, D_CONV)),
        'conv_ln_b': nrm((DEPTH, D_CONV), 0.02),
        'w_pw': nrm((DEPTH, D_CONV, D_CONV), D_CONV ** -0.5),
        'sinks': nrm((DEPTH, N_HEADS), 0.5),
        'w_out': nrm((DEPTH, D_MIX, D_MODEL), BETA * D_MIX ** -0.5),
        'ln2_g': gain((DEPTH, D_MODEL)),
        'ln2_b': nrm((DEPTH, D_MODEL), 0.02),
        'ffn2_w_gate': nrm((DEPTH, D_MODEL, D_FF), D_MODEL ** -0.5),
        'ffn2_w_up': nrm((DEPTH, D_MODEL, D_FF), D_MODEL ** -0.5),
        'ffn2_w_down': nrm((DEPTH, D_FF, D_MODEL), BETA * D_FF ** -0.5),
        'ln3_g': gain((DEPTH, D_MODEL)),
        'ln3_b': nrm((DEPTH, D_MODEL), 0.02),
        'w_ple_gate': nrm((DEPTH, D_MODEL, D_MODEL), D_MODEL ** -0.5),
        'w_ple': nrm((DEPTH, D_PLE, D_MODEL), D_PLE ** -0.5),
    }


def reference(x_prompt, x_sample, p_prompt, p_sample, cache_k, cache_v, state_pool, state_conv,
              rel_bias, ln1_g, ln1_b, ffn1_w_gate, ffn1_w_up, ffn1_w_down, w_in, w_pool,
              pool_scale, w_dw, b_dw, conv_ln_g, conv_ln_b, w_pw, sinks, w_out, ln2_g, ln2_b,
              ffn2_w_gate, ffn2_w_up, ffn2_w_down, ln3_g, ln3_b, w_ple_gate, w_ple):
    prm = dict(rel_bias=rel_bias, ln1_g=ln1_g, ln1_b=ln1_b, ffn1_w_gate=ffn1_w_gate,
               ffn1_w_up=ffn1_w_up, ffn1_w_down=ffn1_w_down, w_in=w_in, w_pool=w_pool,
               pool_scale=pool_scale, w_dw=w_dw, b_dw=b_dw, conv_ln_g=conv_ln_g,
               conv_ln_b=conv_ln_b, w_pw=w_pw, sinks=sinks, w_out=w_out, ln2_g=ln2_g,
               ln2_b=ln2_b, ffn2_w_gate=ffn2_w_gate, ffn2_w_up=ffn2_w_up,
               ffn2_w_down=ffn2_w_down, ln3_g=ln3_g, ln3_b=ln3_b, w_ple_gate=w_ple_gate,
               w_ple=w_ple)
    y_prompt, (new_k_prompt, new_v_prompt, new_pool_prompt, new_conv_prompt) = _trunk(
        x_prompt, p_prompt, 0, None, None, None, None, prm)
    y_sample, (new_k_sample, new_v_sample, new_pool_sample, new_conv_sample) = _trunk(
        x_sample, p_sample, PAST_LEN, cache_k, cache_v, state_pool, state_conv, prm)
    return (y_prompt, y_sample, new_k_prompt, new_v_prompt, new_pool_prompt, new_conv_prompt,
            new_k_sample, new_v_sample, new_pool_sample, new_conv_sample)
```

```python
import functools
import math

import numpy as np
import jax
import jax.numpy as jnp
from jax import lax
from jax.experimental import pallas as pl
from jax.experimental.pallas import tpu as pltpu

D_MODEL = 2048
BATCH = 4
SEQ = 2048
DEPTH = 4
DEC_BATCH = 32
PAST_LEN = 16384
D_POOL = 512
POOL_WINDOWS = (2, 4, 8, 16)
POOL_GROUP = 128
POOL_STATE = 15
D_CONV = 512
CONV_WIDTH = 31
CONV_STATE = 30
D_ATTN = 1024
HEAD_DIM = 64
N_HEADS = 16
N_KV_HEADS = 4
GQA_GROUP = 4
D_KV = 256
WINDOW = 128
ATTN_SCALE = HEAD_DIM ** -0.5
NEG_INF = -1e30
N_BUCKETS = 32
MAX_EXACT = 16
MAX_DISTANCE = 128
D_IN = 3072
D_FF = 5632
D_PLE = 256
LN_EPS = 1e-5
ALPHA = (2 * DEPTH) ** 0.25

OFF_U, OFF_A, OFF_G, OFF_Q, OFF_K, OFF_V = 0, 512, 1024, 1536, 2560, 2816

M_PROMPT = BATCH * SEQ
M_PAD = M_PROMPT + 128
TM = 1040
SEQ_BLOCKS = SEQ // WINDOW
SAMPLE_BLOCK = M_PROMPT // WINDOW

V7X_VMEM_LIMIT = 58 * 1024 * 1024

F32 = jnp.float32
BF16 = jnp.bfloat16


def _params(*sem):
    return pltpu.CompilerParams(dimension_semantics=sem, vmem_limit_bytes=V7X_VMEM_LIMIT)


def _layer_norm_rows(r, g, b):
    mu = jnp.mean(r, axis=-1, keepdims=True)
    d = r - mu
    var = jnp.mean(d * d, axis=-1, keepdims=True)
    return d * lax.rsqrt(var + LN_EPS) * g + b


def _ffn_up_kernel(x_ref, wg_ref, wu_ref, o_ref):
    x = x_ref[...]
    g = jnp.dot(x, wg_ref[...], preferred_element_type=F32)
    u = jnp.dot(x, wu_ref[...], preferred_element_type=F32)
    o_ref[...] = (g * jax.nn.sigmoid(g) * u).astype(o_ref.dtype)


def _ffn_up(xb, wg, wu):
    tf = 512
    return pl.pallas_call(
        _ffn_up_kernel,
        grid=(D_FF // tf, M_PAD // TM),
        in_specs=[
            pl.BlockSpec((TM, D_MODEL), lambda n, m: (m, 0)),
            pl.BlockSpec((D_MODEL, tf), lambda n, m: (0, n)),
            pl.BlockSpec((D_MODEL, tf), lambda n, m: (0, n)),
        ],
        out_specs=pl.BlockSpec((TM, tf), lambda n, m: (m, n)),
        out_shape=jax.ShapeDtypeStruct((M_PAD, D_FF), BF16),
        compiler_params=_params("arbitrary", "arbitrary"),
        name="ffn_up",
    )(xb, wg, wu)


def _proj_ln_kernel(a_ref, w_ref, h_ref, g_ref, b_ref, of_ref, ob_ref, *, scale):
    y = jnp.dot(a_ref[...], w_ref[...], preferred_element_type=F32)
    r = ALPHA * h_ref[...] + scale * y
    o = _layer_norm_rows(r, g_ref[...], b_ref[...])
    of_ref[...] = o
    ob_ref[...] = o.astype(BF16)


def _proj_ln(a, w, hf, g, b, *, scale, tm):
    k = a.shape[1]
    return pl.pallas_call(
        functools.partial(_proj_ln_kernel, scale=scale),
        grid=(M_PAD // tm,),
        in_specs=[
            pl.BlockSpec((tm, k), lambda m: (m, 0)),
            pl.BlockSpec((k, D_MODEL), lambda m: (0, 0), pipeline_mode=pl.Buffered(1)),
            pl.BlockSpec((tm, D_MODEL), lambda m: (m, 0)),
            pl.BlockSpec((1, D_MODEL), lambda m: (0, 0)),
            pl.BlockSpec((1, D_MODEL), lambda m: (0, 0)),
        ],
        out_specs=[
            pl.BlockSpec((tm, D_MODEL), lambda m: (m, 0)),
            pl.BlockSpec((tm, D_MODEL), lambda m: (m, 0)),
        ],
        out_shape=[
            jax.ShapeDtypeStruct((M_PAD, D_MODEL), F32),
            jax.ShapeDtypeStruct((M_PAD, D_MODEL), BF16),
        ],
        compiler_params=_params("arbitrary"),
        name="proj_ln_k%d" % k,
    )(a, w, hf, g, b)


def _w_in_kernel(x_ref, w_ref, o_ref):
    o_ref[...] = jnp.dot(x_ref[...], w_ref[...], preferred_element_type=F32)


def _w_in(xb, w):
    tn = 1024
    return pl.pallas_call(
        _w_in_kernel,
        grid=(D_IN // tn, M_PAD // TM),
        in_specs=[
            pl.BlockSpec((TM, D_MODEL), lambda n, m: (m, 0)),
            pl.BlockSpec((D_MODEL, tn), lambda n, m: (0, n)),
        ],
        out_specs=pl.BlockSpec((TM, tn), lambda n, m: (m, n)),
        out_shape=jax.ShapeDtypeStruct((M_PAD, D_IN), F32),
        compiler_params=_params("arbitrary", "arbitrary"),
        name="w_in",
    )(xb, w)


def _ple_kernel(xb_ref, p_ref, wg_ref, wp_ref, h_ref, of_ref, ob_ref):
    gate = jax.nn.sigmoid(jnp.dot(xb_ref[...], wg_ref[...], preferred_element_type=F32))
    emb = jnp.dot(p_ref[...], wp_ref[...], preferred_element_type=F32)
    o = h_ref[...] + gate * emb
    of_ref[...] = o
    ob_ref[...] = o.astype(BF16)


def _ple(hb, hf, pb, wg, wp):
    tn = 1024
    return pl.pallas_call(
        _ple_kernel,
        grid=(D_MODEL // tn, M_PAD // TM),
        in_specs=[
            pl.BlockSpec((TM, D_MODEL), lambda n, m: (m, 0)),
            pl.BlockSpec((TM, D_PLE), lambda n, m: (m, 0)),
            pl.BlockSpec((D_MODEL, tn), lambda n, m: (0, n)),
            pl.BlockSpec((D_PLE, tn), lambda n, m: (0, n)),
            pl.BlockSpec((TM, tn), lambda n, m: (m, n)),
        ],
        out_specs=[
            pl.BlockSpec((TM, tn), lambda n, m: (m, n)),
            pl.BlockSpec((TM, tn), lambda n, m: (m, n)),
        ],
        out_shape=[
            jax.ShapeDtypeStruct((M_PAD, D_MODEL), F32),
            jax.ShapeDtypeStruct((M_PAD, D_MODEL), BF16),
        ],
        compiler_params=_params("arbitrary", "arbitrary"),
        name="ple",
    )(hb, pb, wg, wp, hf)


def _conv_tail(y, bdw, lg, lb, wpw):
    y = _layer_norm_rows(y + bdw, lg, lb)
    y = y * jax.nn.sigmoid(y)
    return jnp.dot(y.astype(BF16), wpw, preferred_element_type=F32)


def _mix_prompt_kernel(zc_ref, zp_ref, wpool_ref, pscale_ref, wdw_ref, bdw_ref, lg_ref, lb_ref,
                       wpw_ref, bm_ref, sink_ref, o_ref, ctail_ref, ubuf, cbuf, ybuf):
    i = pl.program_id(1)
    has_prev = i > 0
    T = WINDOW

    u_cur = zc_ref[:, OFF_U:OFF_U + D_POOL]
    ubuf[0:16, :] = jnp.where(has_prev, zp_ref[T - 16:T, OFF_U:OFF_U + D_POOL], 0.0)
    ubuf[16:16 + T, :] = u_cur
    pos = i * T + lax.broadcasted_iota(jnp.int32, (T, 1), 0)
    for g, w in enumerate(POOL_WINDOWS):
        sl = slice(g * POOL_GROUP, (g + 1) * POOL_GROUP)
        ws = ubuf[16:16 + T, sl]
        for k in range(1, w):
            ws = ws + ubuf[16 - k:16 - k + T, sl]
        cnt = jnp.minimum(pos + 1, w).astype(F32)
        d = ws / cnt - u_cur[:, sl]
        y = jnp.dot(d.astype(BF16), wpool_ref[g], preferred_element_type=F32)
        o_ref[:, sl] = (y * pscale_ref[:, sl]).astype(BF16)

    c_cur = zc_ref[:, OFF_A:OFF_A + D_CONV] * jax.nn.sigmoid(zc_ref[:, OFF_G:OFF_G + D_CONV])
    c_prev = (zp_ref[T - 32:T, OFF_A:OFF_A + D_CONV]
              * jax.nn.sigmoid(zp_ref[T - 32:T, OFF_G:OFF_G + D_CONV]))
    cbuf[0:32, :] = jnp.where(has_prev, c_prev, 0.0)
    cbuf[32:32 + T, :] = c_cur
    ctail_ref[...] = c_cur[T - 32:T, :]
    for s in range(D_CONV // 128):
        sl = slice(s * 128, (s + 1) * 128)
        acc = cbuf[2:2 + T, sl] * wdw_ref[0:1, sl]
        for j in range(1, CONV_WIDTH):
            acc = acc + cbuf[2 + j:2 + j + T, sl] * wdw_ref[j:j + 1, sl]
        ybuf[:, sl] = acc
    yc = _conv_tail(ybuf[...], bdw_ref[...], lg_ref[...], lb_ref[...], wpw_ref[...])
    o_ref[:, D_POOL:D_POOL + D_CONV] = yc.astype(BF16)

    col = lax.broadcasted_iota(jnp.int32, (GQA_GROUP * T, 2 * T), 1)
    first_block_mask = jnp.logical_or(has_prev, col >= T)
    for j in range(N_KV_HEADS):
        ks = slice(OFF_K + j * HEAD_DIM, OFF_K + (j + 1) * HEAD_DIM)
        vs = slice(OFF_V + j * HEAD_DIM, OFF_V + (j + 1) * HEAD_DIM)
        kb = jnp.concatenate([zp_ref[:, ks], zc_ref[:, ks]], axis=0).astype(BF16)
        vb = jnp.concatenate([zp_ref[:, vs], zc_ref[:, vs]], axis=0).astype(BF16)
        q4 = jnp.concatenate(
            [zc_ref[:, OFF_Q + (j * GQA_GROUP + g) * HEAD_DIM:OFF_Q + (j * GQA_GROUP + g + 1) * HEAD_DIM]
             for g in range(GQA_GROUP)], axis=0)
        q4 = (q4 * ATTN_SCALE).astype(BF16)
        s = lax.dot_general(q4, kb, (((1,), (1,)), ((), ())), preferred_element_type=F32)
        s = jnp.where(first_block_mask, s + bm_ref[j], NEG_INF)
        sink = sink_ref[j]
        m = jnp.maximum(jnp.max(s, axis=-1, keepdims=True), sink)
        e = jnp.exp(s - m)
        denom = jnp.sum(e, axis=-1, keepdims=True) + jnp.exp(sink - m)
        o4 = jnp.dot(e.astype(BF16), vb, preferred_element_type=F32) / denom
        for g in range(GQA_GROUP):
            h = j * GQA_GROUP + g
            c0 = D_POOL + D_CONV + h * HEAD_DIM
            o_ref[:, c0:c0 + HEAD_DIM] = o4[g * T:(g + 1) * T, :].astype(BF16)


def _mix_prompt(z, wpool, pscale, wdw, bdw, lg, lb, wpw, bm, sink_rows):
    cur = lambda b, i: (b * SEQ_BLOCKS + i, 0)
    prev = lambda b, i: (b * SEQ_BLOCKS + jnp.maximum(i - 1, 0), 0)
    const2 = lambda b, i: (0, 0)
    const3 = lambda b, i: (0, 0, 0)
    return pl.pallas_call(
        _mix_prompt_kernel,
        grid=(BATCH, SEQ_BLOCKS),
        in_specs=[
            pl.BlockSpec((WINDOW, D_IN), cur),
            pl.BlockSpec((WINDOW, D_IN), prev),
            pl.BlockSpec((len(POOL_WINDOWS), POOL_GROUP, POOL_GROUP), const3),
            pl.BlockSpec((1, D_POOL), const2),
            pl.BlockSpec((CONV_WIDTH, D_CONV), const2),
            pl.BlockSpec((1, D_CONV), const2),
            pl.BlockSpec((1, D_CONV), const2),
            pl.BlockSpec((1, D_CONV), const2),
            pl.BlockSpec((D_CONV, D_CONV), const2),
            pl.BlockSpec((N_KV_HEADS, GQA_GROUP * WINDOW, 2 * WINDOW), const3),
            pl.BlockSpec((N_KV_HEADS, GQA_GROUP * WINDOW, 1), const3),
        ],
        out_specs=[
            pl.BlockSpec((WINDOW, D_MODEL), cur),
            pl.BlockSpec((None, 32, D_CONV), lambda b, i: (b, 0, 0)),
        ],
        out_shape=[
            jax.ShapeDtypeStruct((M_PAD, D_MODEL), BF16),
            jax.ShapeDtypeStruct((BATCH, 32, D_CONV), F32),
        ],
        scratch_shapes=[
            pltpu.VMEM((16 + WINDOW, D_POOL), F32),
            pltpu.VMEM((32 + WINDOW, D_CONV), F32),
            pltpu.VMEM((WINDOW, D_CONV), F32),
        ],
        compiler_params=_params("arbitrary", "arbitrary"),
        name="mix_prompt",
    )(z, z, wpool, pscale, wdw, bdw, lg, lb, wpw, bm, sink_rows)


def _attn_sample_kernel(q_ref, kn_ref, vn_ref, ck_ref, cv_ref, bc_ref, bn_ref, sink_ref, o_ref):
    row_head = lax.broadcasted_iota(jnp.int32, (N_HEADS, D_KV), 0) // GQA_GROUP
    col_head = lax.broadcasted_iota(jnp.int32, (N_HEADS, D_KV), 1) // HEAD_DIM
    own = row_head == col_head
    sink = sink_ref[...]

    def body(b, carry):
        q = q_ref[b] * ATTN_SCALE
        qx = jnp.where(own, jnp.concatenate([q] * N_KV_HEADS, axis=1), 0.0)
        s_c = lax.dot_general(qx.astype(BF16), ck_ref[b].astype(BF16), (((1,), (1,)), ((), ())),
                              preferred_element_type=F32) + bc_ref[...]
        k_new = kn_ref[pl.ds(b, 1), :]
        s_n = jnp.sum(qx * k_new, axis=-1, keepdims=True) + bn_ref[...]
        m = jnp.maximum(jnp.maximum(jnp.max(s_c, axis=-1, keepdims=True), s_n), sink)
        e_c = jnp.exp(s_c - m)
        e_n = jnp.exp(s_n - m)
        denom = jnp.sum(e_c, axis=-1, keepdims=True) + e_n + jnp.exp(sink - m)
        o = jnp.dot(e_c.astype(BF16), cv_ref[b].astype(BF16), preferred_element_type=F32)
        o = (o + e_n * vn_ref[pl.ds(b, 1), :]) / denom
        o = jnp.where(own, o, 0.0)
        acc = o[:, 0:HEAD_DIM]
        for j in range(1, N_KV_HEADS):
            acc = acc + o[:, j * HEAD_DIM:(j + 1) * HEAD_DIM]
        o_ref[b] = acc
        return carry

    lax.fori_loop(0, DEC_BATCH, body, 0)


def _attn_sample(q3, k_new, v_new, ck, cv, bias_cache, bias_new, sink_col):
    full = lambda shape: pl.BlockSpec(shape, lambda: (0,) * len(shape))
    return pl.pallas_call(
        _attn_sample_kernel,
        grid=(),
        in_specs=[
            full((DEC_BATCH, N_HEADS, HEAD_DIM)),
            full((DEC_BATCH, D_KV)),
            full((DEC_BATCH, D_KV)),
            full((DEC_BATCH, WINDOW, D_KV)),
            full((DEC_BATCH, WINDOW, D_KV)),
            full((N_HEADS, WINDOW)),
            full((N_HEADS, 1)),
            full((N_HEADS, 1)),
        ],
        out_specs=full((DEC_BATCH, N_HEADS, HEAD_DIM)),
        out_shape=jax.ShapeDtypeStruct((DEC_BATCH, N_HEADS, HEAD_DIM), F32),
        compiler_params=pltpu.CompilerParams(vmem_limit_bytes=V7X_VMEM_LIMIT),
        name="attn_sample",
    )(q3, k_new, v_new, ck, cv, bias_cache, bias_new, sink_col)


def _mix_sample_kernel(z_ref, at_ref, sp_ref, sc_ref, wpool_ref, pscale_ref, wdw_ref, bdw_ref,
                       lg_ref, lb_ref, wpw_ref, m_hbm_ref, o_ref, cnew_ref):
    del m_hbm_ref
    B = DEC_BATCH
    o_ref[...] = jnp.zeros(o_ref.shape, o_ref.dtype)

    u_new = z_ref[0:B, OFF_U:OFF_U + D_POOL]
    for g, w in enumerate(POOL_WINDOWS):
        sl = slice(g * POOL_GROUP, (g + 1) * POOL_GROUP)
        ws = u_new[:, sl]
        for k in range(1, w):
            ws = ws + sp_ref[POOL_STATE - k, :, sl]
        cnt = float(min(PAST_LEN + 1, w))
        d = ws / cnt - u_new[:, sl]
        y = jnp.dot(d.astype(BF16), wpool_ref[g], preferred_element_type=F32)
        o_ref[0:B, sl] = (y * pscale_ref[:, sl]).astype(BF16)

    c_new = z_ref[0:B, OFF_A:OFF_A + D_CONV] * jax.nn.sigmoid(z_ref[0:B, OFF_G:OFF_G + D_CONV])
    cnew_ref[...] = c_new
    acc = c_new * wdw_ref[CONV_STATE:CONV_STATE + 1, :]
    for j in range(CONV_STATE):
        acc = acc + sc_ref[j] * wdw_ref[j:j + 1, :]
    yc = _conv_tail(acc, bdw_ref[...], lg_ref[...], lb_ref[...], wpw_ref[...])
    o_ref[0:B, D_POOL:D_POOL + D_CONV] = yc.astype(BF16)

    o_ref[0:B, D_POOL + D_CONV:D_MODEL] = at_ref[...].astype(BF16)


def _mix_sample(z, attn, sp_t, sc_t, wpool, pscale, wdw, bdw, lg, lb, wpw, m_all):
    c2 = lambda i: (0, 0)
    c3 = lambda i: (0, 0, 0)
    return pl.pallas_call(
        _mix_sample_kernel,
        grid=(1,),
        in_specs=[
            pl.BlockSpec((WINDOW, D_IN), lambda i: (SAMPLE_BLOCK, 0)),
            pl.BlockSpec((DEC_BATCH, D_ATTN), c2),
            pl.BlockSpec((POOL_STATE, DEC_BATCH, D_POOL), c3),
            pl.BlockSpec((CONV_STATE, DEC_BATCH, D_CONV), c3),
            pl.BlockSpec((len(POOL_WINDOWS), POOL_GROUP, POOL_GROUP), c3),
            pl.BlockSpec((1, D_POOL), c2),
            pl.BlockSpec((CONV_WIDTH, D_CONV), c2),
            pl.BlockSpec((1, D_CONV), c2),
            pl.BlockSpec((1, D_CONV), c2),
            pl.BlockSpec((1, D_CONV), c2),
            pl.BlockSpec((D_CONV, D_CONV), c2),
            pl.BlockSpec(memory_space=pl.ANY),
        ],
        out_specs=[
            pl.BlockSpec((WINDOW, D_MODEL), lambda i: (SAMPLE_BLOCK, 0)),
            pl.BlockSpec((DEC_BATCH, D_CONV), c2),
        ],
        out_shape=[
            jax.ShapeDtypeStruct((M_PAD, D_MODEL), BF16),
            jax.ShapeDtypeStruct((DEC_BATCH, D_CONV), F32),
        ],
        input_output_aliases={11: 0},
        compiler_params=_params("arbitrary"),
        name="mix_sample",
    )(z, attn, sp_t, sc_t, wpool, pscale, wdw, bdw, lg, lb, wpw, m_all)


def _bucket_of_distance():
    n = np.arange(WINDOW)
    nf = np.maximum(n, 1).astype(np.float32)
    log_b = MAX_EXACT + (np.log(nf / MAX_EXACT) / math.log(MAX_DISTANCE / MAX_EXACT)
                         * (N_BUCKETS - MAX_EXACT)).astype(np.int32)
    return np.where(n < MAX_EXACT, n, np.minimum(log_b, N_BUCKETS - 1)).astype(np.int32)


def _bias_tables(rel_bias):
    bias_by_dist = rel_bias.astype(F32)[_bucket_of_distance()]
    q = np.arange(WINDOW)[:, None]
    s = np.arange(2 * WINDOW)[None, :]
    dist = WINDOW + q - s
    valid = (dist >= 0) & (dist < WINDOW)
    bm = jnp.where(valid[:, :, None], bias_by_dist[np.clip(dist, 0, WINDOW - 1)], NEG_INF)
    bm = jnp.transpose(bm, (2, 0, 1)).reshape(N_KV_HEADS, GQA_GROUP * WINDOW, 2 * WINDOW)
    ds = WINDOW - np.arange(WINDOW)
    bias_cache = jnp.where((ds < WINDOW)[:, None], bias_by_dist[np.clip(ds, 0, WINDOW - 1)], NEG_INF).T
    bias_new = bias_by_dist[0][:, None]
    return bm, bias_cache, bias_new


def kernel(x_prompt, x_sample, p_prompt, p_sample, cache_k, cache_v, state_pool, state_conv, rel_bias, ln1_g, ln1_b, ffn1_w_gate, ffn1_w_up, ffn1_w_down, w_in, w_pool, pool_scale, w_dw, b_dw, conv_ln_g, conv_ln_b, w_pw, sinks, w_out, ln2_g, ln2_b, ffn2_w_gate, ffn2_w_up, ffn2_w_down, ln3_g, ln3_b, w_ple_gate, w_ple):
    pad_rows = M_PAD - M_PROMPT - DEC_BATCH
    hf = jnp.concatenate([x_prompt.reshape(M_PROMPT, D_MODEL), x_sample.reshape(DEC_BATCH, D_MODEL),
                          jnp.zeros((pad_rows, D_MODEL), F32)], axis=0)
    hb = hf.astype(BF16)
    p_all = jnp.concatenate([p_prompt.reshape(DEPTH, M_PROMPT, D_PLE),
                             p_sample.reshape(DEPTH, DEC_BATCH, D_PLE),
                             jnp.zeros((DEPTH, pad_rows, D_PLE), F32)], axis=1).astype(BF16)

    bm, bias_cache, bias_new = _bias_tables(rel_bias)
    sinks = sinks.astype(F32)
    sink_rows = jnp.repeat(sinks.reshape(DEPTH, N_KV_HEADS, GQA_GROUP), WINDOW, axis=2)[..., None]
    sink_col = sinks[..., None]

    ck = cache_k.reshape(DEPTH, DEC_BATCH, WINDOW, D_KV)
    cv = cache_v.reshape(DEPTH, DEC_BATCH, WINDOW, D_KV)
    sp_t = jnp.transpose(state_pool, (0, 2, 1, 3))
    sc_t = jnp.transpose(state_conv, (0, 2, 1, 3))

    row = lambda v, l: v[l][None, :].astype(F32)
    nk_p, nv_p, npool_p, nconv_p = [], [], [], []
    nk_s, nv_s, npool_s, nconv_s = [], [], [], []

    for l in range(DEPTH):
        a = _ffn_up(hb, ffn1_w_gate[l].astype(BF16), ffn1_w_up[l].astype(BF16))
        hf, hb = _proj_ln(a, ffn1_w_down[l].astype(BF16), hf, row(ln1_g, l), row(ln1_b, l),
                          scale=0.5, tm=320)

        z = _w_in(hb, w_in[l].astype(BF16))
        wpool = w_pool[l].astype(BF16)
        wpw = w_pw[l].astype(BF16)
        mixer_w = (wpool, row(pool_scale, l), w_dw[l].astype(F32), row(b_dw, l),
                   row(conv_ln_g, l), row(conv_ln_b, l), wpw)
        m_all, ctail = _mix_prompt(z, *mixer_w, bm, sink_rows[l])

        zs = z[M_PROMPT:M_PROMPT + DEC_BATCH]
        k_new = zs[:, OFF_K:OFF_K + D_KV]
        v_new = zs[:, OFF_V:OFF_V + D_KV]
        q3 = zs[:, OFF_Q:OFF_Q + D_ATTN].reshape(DEC_BATCH, N_HEADS, HEAD_DIM)
        attn_s = _attn_sample(q3, k_new, v_new, ck[l], cv[l], bias_cache, bias_new, sink_col[l])
        m_all, c_new = _mix_sample(z, attn_s.reshape(DEC_BATCH, D_ATTN), sp_t[l], sc_t[l],
                                   *mixer_w, m_all)

        hf, hb = _proj_ln(m_all, w_out[l].astype(BF16), hf, row(ln2_g, l), row(ln2_b, l),
                          scale=1.0, tm=640)

        a = _ffn_up(hb, ffn2_w_gate[l].astype(BF16), ffn2_w_up[l].astype(BF16))
        hf, hb = _proj_ln(a, ffn2_w_down[l].astype(BF16), hf, row(ln3_g, l), row(ln3_b, l),
                          scale=0.5, tm=320)
        hf, hb = _ple(hb, hf, p_all[l], w_ple_gate[l].astype(BF16), w_ple[l].astype(BF16))

        zp = z[:M_PROMPT].reshape(BATCH, SEQ, D_IN)
        nk_p.append(zp[:, SEQ - WINDOW:, OFF_K:OFF_K + D_KV].reshape(BATCH, WINDOW, N_KV_HEADS, HEAD_DIM))
        nv_p.append(zp[:, SEQ - WINDOW:, OFF_V:OFF_V + D_KV].reshape(BATCH, WINDOW, N_KV_HEADS, HEAD_DIM))
        npool_p.append(zp[:, SEQ - POOL_STATE:, OFF_U:OFF_U + D_POOL])
        nconv_p.append(ctail[:, 32 - CONV_STATE:, :])
        nk_s.append(jnp.concatenate(
            [cache_k[l][:, 1:], k_new.reshape(DEC_BATCH, 1, N_KV_HEADS, HEAD_DIM)], axis=1))
        nv_s.append(jnp.concatenate(
            [cache_v[l][:, 1:], v_new.reshape(DEC_BATCH, 1, N_KV_HEADS, HEAD_DIM)], axis=1))
        npool_s.append(jnp.concatenate(
            [state_pool[l][:, 1:], zs[:, None, OFF_U:OFF_U + D_POOL]], axis=1))
        nconv_s.append(jnp.concatenate([state_conv[l][:, 1:], c_new[:, None, :]], axis=1))

    y_prompt = hf[:M_PROMPT].reshape(BATCH, SEQ, D_MODEL)
    y_sample = hf[M_PROMPT:M_PROMPT + DEC_BATCH].reshape(DEC_BATCH, 1, D_MODEL)
    st = lambda xs: jnp.stack(xs)
    return (y_prompt, y_sample, st(nk_p), st(nv_p), st(npool_p), st(nconv_p),
            st(nk_s), st(nv_s), st(npool_s), st(nconv_s))
```

```python
import functools
import math

import numpy as np
import jax
import jax.numpy as jnp
from jax import lax
from jax.experimental import pallas as pl
from jax.experimental.pallas import tpu as pltpu

D_MODEL = 2048
BATCH = 4
SEQ = 2048
DEPTH = 4
DEC_BATCH = 32
PAST_LEN = 16384
D_POOL = 512
POOL_WINDOWS = (2, 4, 8, 16)
POOL_GROUP = 128
POOL_STATE = 15
D_CONV = 512
CONV_WIDTH = 31
CONV_STATE = 30
D_ATTN = 1024
HEAD_DIM = 64
N_HEADS = 16
N_KV_HEADS = 4
GQA_GROUP = 4
D_KV = 256
WINDOW = 128
ATTN_SCALE = HEAD_DIM ** -0.5
NEG_INF = -1e30
N_BUCKETS = 32
MAX_EXACT = 16
MAX_DISTANCE = 128
D_IN = 3072
D_FF = 5632
D_PLE = 256
LN_EPS = 1e-5
ALPHA = (2 * DEPTH) ** 0.25

OFF_U, OFF_A, OFF_G, OFF_Q, OFF_K, OFF_V = 0, 512, 1024, 1536, 2560, 2816

LANES = 128
M_PROMPT = BATCH * SEQ
M_PAD = M_PROMPT + 128
TM = 1040
SEQ_BLOCKS = SEQ // WINDOW
SAMPLE_BLOCK = M_PROMPT // WINDOW
POOL_HALO = 16
CONV_HALO = 32

V7X_VMEM_LIMIT = 58 * 1024 * 1024

F32 = jnp.float32
BF16 = jnp.bfloat16


def _params(*sem):
    return pltpu.CompilerParams(dimension_semantics=sem, vmem_limit_bytes=V7X_VMEM_LIMIT)


def _layer_norm_rows(r, g, b):
    mu = jnp.mean(r, axis=-1, keepdims=True)
    d = r - mu
    var = jnp.mean(d * d, axis=-1, keepdims=True)
    return d * lax.rsqrt(var + LN_EPS) * g + b


def _ffn_up_kernel(x_ref, wg_ref, wu_ref, wd_ref, o_ref, wdb_ref, wgb, wub):
    @pl.when(pl.program_id(1) == 0)
    def _():
        wgb[...] = wg_ref[...].astype(BF16)
        wub[...] = wu_ref[...].astype(BF16)
        wdb_ref[...] = wd_ref[...].astype(BF16)

    x = x_ref[...]
    g = jnp.dot(x, wgb[...], preferred_element_type=F32)
    u = jnp.dot(x, wub[...], preferred_element_type=F32)
    o_ref[...] = (g * jax.nn.sigmoid(g) * u).astype(o_ref.dtype)


def _ffn_up(xb, wg, wu, wd, l):
    tf = 512
    return pl.pallas_call(
        _ffn_up_kernel,
        grid=(D_FF // tf, M_PAD // TM),
        in_specs=[
            pl.BlockSpec((TM, D_MODEL), lambda n, m: (m, 0)),
            pl.BlockSpec((None, D_MODEL, tf), lambda n, m: (l, 0, n)),
            pl.BlockSpec((None, D_MODEL, tf), lambda n, m: (l, 0, n)),
            pl.BlockSpec((None, tf, D_MODEL), lambda n, m: (l, n, 0)),
        ],
        out_specs=[
            pl.BlockSpec((TM, tf), lambda n, m: (m, n)),
            pl.BlockSpec((tf, D_MODEL), lambda n, m: (n, 0)),
        ],
        out_shape=[
            jax.ShapeDtypeStruct((M_PAD, D_FF), BF16),
            jax.ShapeDtypeStruct((D_FF, D_MODEL), BF16),
        ],
        scratch_shapes=[pltpu.VMEM((D_MODEL, tf), BF16), pltpu.VMEM((D_MODEL, tf), BF16)],
        compiler_params=_params("arbitrary", "arbitrary"),
        name="ffn_up",
    )(xb, wg, wu, wd)


def _proj_ln_kernel(a_ref, w_ref, h_ref, g_ref, b_ref, of_ref, ob_ref, *, scale):
    y = jnp.dot(a_ref[...], w_ref[...], preferred_element_type=F32)
    r = ALPHA * h_ref[...] + scale * y
    o = _layer_norm_rows(r, g_ref[...], b_ref[...])
    of_ref[...] = o
    ob_ref[...] = o.astype(BF16)


def _proj_ln(a, w, hf, g, b, *, scale, tm):
    k = a.shape[1]
    return pl.pallas_call(
        functools.partial(_proj_ln_kernel, scale=scale),
        grid=(M_PAD // tm,),
        in_specs=[
            pl.BlockSpec((tm, k), lambda m: (m, 0)),
            pl.BlockSpec((k, D_MODEL), lambda m: (0, 0), pipeline_mode=pl.Buffered(1)),
            pl.BlockSpec((tm, D_MODEL), lambda m: (m, 0)),
            pl.BlockSpec((1, D_MODEL), lambda m: (0, 0)),
            pl.BlockSpec((1, D_MODEL), lambda m: (0, 0)),
        ],
        out_specs=[
            pl.BlockSpec((tm, D_MODEL), lambda m: (m, 0)),
            pl.BlockSpec((tm, D_MODEL), lambda m: (m, 0)),
        ],
        out_shape=[
            jax.ShapeDtypeStruct((M_PAD, D_MODEL), F32),
            jax.ShapeDtypeStruct((M_PAD, D_MODEL), BF16),
        ],
        compiler_params=_params("arbitrary"),
        name="proj_ln_k%d" % k,
    )(a, w, hf, g, b)


def _w_in_kernel(x_ref, w_ref, wo_ref, o_ref, wob_ref, wb):
    @pl.when(pl.program_id(1) == 0)
    def _():
        wb[...] = w_ref[...].astype(BF16)
        wob_ref[...] = wo_ref[...].astype(BF16)

    o_ref[...] = jnp.dot(x_ref[...], wb[...], preferred_element_type=F32)


def _w_in(xb, w, wo, l):
    tn = 768
    n_steps = D_IN // tn
    to = D_MODEL // n_steps
    return pl.pallas_call(
        _w_in_kernel,
        grid=(n_steps, M_PAD // TM),
        in_specs=[
            pl.BlockSpec((TM, D_MODEL), lambda n, m: (m, 0)),
            pl.BlockSpec((None, D_MODEL, tn), lambda n, m: (l, 0, n)),
            pl.BlockSpec((None, to, D_MODEL), lambda n, m: (l, n, 0)),
        ],
        out_specs=[
            pl.BlockSpec((TM, tn), lambda n, m: (m, n)),
            pl.BlockSpec((to, D_MODEL), lambda n, m: (n, 0)),
        ],
        out_shape=[
            jax.ShapeDtypeStruct((M_PAD, D_IN), F32),
            jax.ShapeDtypeStruct((D_MODEL, D_MODEL), BF16),
        ],
        scratch_shapes=[pltpu.VMEM((D_MODEL, tn), BF16)],
        compiler_params=_params("arbitrary", "arbitrary"),
        name="w_in",
    )(xb, w, wo)


def _ple_kernel(xb_ref, p_ref, wg_ref, wp_ref, h_ref, of_ref, ob_ref, wgb, wpb):
    @pl.when(pl.program_id(1) == 0)
    def _():
        wgb[...] = wg_ref[...].astype(BF16)
        wpb[...] = wp_ref[...].astype(BF16)

    gate = jax.nn.sigmoid(jnp.dot(xb_ref[...], wgb[...], preferred_element_type=F32))
    emb = jnp.dot(p_ref[...], wpb[...], preferred_element_type=F32)
    o = h_ref[...] + gate * emb
    of_ref[...] = o
    ob_ref[...] = o.astype(BF16)


def _ple(hb, hf, pb, wg, wp, l):
    tn = 512
    return pl.pallas_call(
        _ple_kernel,
        grid=(D_MODEL // tn, M_PAD // TM),
        in_specs=[
            pl.BlockSpec((TM, D_MODEL), lambda n, m: (m, 0)),
            pl.BlockSpec((None, TM, D_PLE), lambda n, m: (l, m, 0)),
            pl.BlockSpec((None, D_MODEL, tn), lambda n, m: (l, 0, n)),
            pl.BlockSpec((None, D_PLE, tn), lambda n, m: (l, 0, n)),
            pl.BlockSpec((TM, tn), lambda n, m: (m, n)),
        ],
        out_specs=[
            pl.BlockSpec((TM, tn), lambda n, m: (m, n)),
            pl.BlockSpec((TM, tn), lambda n, m: (m, n)),
        ],
        out_shape=[
            jax.ShapeDtypeStruct((M_PAD, D_MODEL), F32),
            jax.ShapeDtypeStruct((M_PAD, D_MODEL), BF16),
        ],
        scratch_shapes=[pltpu.VMEM((D_MODEL, tn), BF16), pltpu.VMEM((D_PLE, tn), BF16)],
        compiler_params=_params("arbitrary", "arbitrary"),
        name="ple",
    )(hb, pb, wg, wp, hf)


def _conv_tail(y, bdw, lg, lb, wpw):
    y = _layer_norm_rows(y + bdw, lg, lb)
    y = y * jax.nn.sigmoid(y)
    return jnp.dot(y.astype(BF16), wpw, preferred_element_type=F32)


def _mix_prompt_kernel(sink_ref, z_ref, wpool_ref, pscale_ref, wdw_ref, bdw_ref, lg_ref, lb_ref,
                       wpw_ref, bm_ref, o_ref, ctail_ref, utail_ref, kvtail_ref,
                       ubuf, cbuf, ybuf, kd, vd):
    i = pl.program_id(1)
    T = WINDOW
    n_slab = D_CONV // LANES

    @pl.when(i == 0)
    def _():
        ubuf[:, 0:POOL_HALO, :] = jnp.zeros((n_slab, POOL_HALO, LANES), F32)
        cbuf[:, 0:CONV_HALO, :] = jnp.zeros((n_slab, CONV_HALO, LANES), F32)
        kd[:, 0:T, :] = jnp.zeros((N_KV_HEADS, T, LANES), BF16)
        vd[:, 0:T, :] = jnp.zeros((N_KV_HEADS, T, LANES), BF16)

    @pl.when(i > 0)
    def _():
        ubuf[:, 0:POOL_HALO, :] = ubuf[:, T:T + POOL_HALO, :]
        cbuf[:, 0:CONV_HALO, :] = cbuf[:, T:T + CONV_HALO, :]
        kd[:, 0:T, :] = kd[:, T:2 * T, :]
        vd[:, 0:T, :] = vd[:, T:2 * T, :]

    @pl.when(i == SEQ_BLOCKS - 1)
    def _():
        utail_ref[...] = z_ref[T - POOL_HALO:T, OFF_U:OFF_U + D_POOL]
        kvtail_ref[...] = z_ref[:, OFF_K:OFF_K + 2 * D_KV]

    pos = i * T + lax.broadcasted_iota(jnp.int32, (T, 1), 0)
    for g, w in enumerate(POOL_WINDOWS):
        sl = slice(g * POOL_GROUP, (g + 1) * POOL_GROUP)
        u_cur = z_ref[:, OFF_U + g * POOL_GROUP:OFF_U + (g + 1) * POOL_GROUP]
        ubuf[g, POOL_HALO:POOL_HALO + T, :] = u_cur
        ws = u_cur
        for k in range(1, w):
            ws = ws + ubuf[g, POOL_HALO - k:POOL_HALO - k + T, :]
        cnt = jnp.minimum(pos + 1, w).astype(F32)
        d = ws / cnt - u_cur
        y = jnp.dot(d.astype(BF16), wpool_ref[g], preferred_element_type=F32)
        o_ref[:, sl] = (y * pscale_ref[:, sl]).astype(BF16)

    for s in range(n_slab):
        sl = slice(s * LANES, (s + 1) * LANES)
        c_cur = (z_ref[:, OFF_A + s * LANES:OFF_A + (s + 1) * LANES]
                 * jax.nn.sigmoid(z_ref[:, OFF_G + s * LANES:OFF_G + (s + 1) * LANES]))
        cbuf[s, CONV_HALO:CONV_HALO + T, :] = c_cur
        ctail_ref[:, sl] = c_cur[T - CONV_HALO:T, :]
        r0 = CONV_HALO - CONV_STATE
        acc = cbuf[s, r0:r0 + T, :] * wdw_ref[0:1, sl]
        for j in range(1, CONV_WIDTH):
            acc = acc + cbuf[s, r0 + j:r0 + j + T, :] * wdw_ref[j:j + 1, sl]
        ybuf[:, sl] = acc
    yc = _conv_tail(ybuf[...], bdw_ref[...], lg_ref[...], lb_ref[...], wpw_ref[...])
    o_ref[:, D_POOL:D_POOL + D_CONV] = yc.astype(BF16)

    lo = lax.broadcasted_iota(jnp.int32, (T, LANES), 1) < HEAD_DIM
    hi = jnp.logical_not(lo)
    for m in range(N_KV_HEADS // 2):
        for src_off, dst in ((OFF_K, kd), (OFF_V, vd)):
            pair = z_ref[:, src_off + m * LANES:src_off + (m + 1) * LANES]
            swapped = pltpu.roll(pair, HEAD_DIM, axis=1)
            dst[2 * m, T:2 * T, :] = jnp.where(lo, pair, swapped).astype(BF16)
            dst[2 * m + 1, T:2 * T, :] = jnp.where(lo, swapped, pair).astype(BF16)
    for ip in range(N_HEADS // 2):
        j = (2 * ip) // GQA_GROUP
        qpair = z_ref[:, OFF_Q + ip * LANES:OFF_Q + (ip + 1) * LANES] * ATTN_SCALE
        kk = kd[j]
        vv = vd[j]
        res = []
        for half, keep in enumerate((lo, hi)):
            h = 2 * ip + half
            qh = jnp.where(keep, qpair, 0.0).astype(BF16)
            s = lax.dot_general(qh, kk, (((1,), (1,)), ((), ())), preferred_element_type=F32)
            s = s + bm_ref[h]
            sink = sink_ref[h]
            mx = jnp.maximum(jnp.max(s, axis=-1, keepdims=True), sink)
            e = jnp.exp(s - mx)
            den = jnp.sum(e, axis=-1, keepdims=True) + jnp.exp(sink - mx)
            res.append(jnp.dot(e.astype(BF16), vv, preferred_element_type=F32) / den)
        c0 = D_POOL + D_CONV + ip * LANES
        o_ref[:, c0:c0 + LANES] = jnp.where(lo, res[0], res[1]).astype(BF16)


def _mix_prompt(z, sink, wpool, pscale, wdw, bdw, lg, lb, wpw, bm2):
    cur = lambda b, i: (b * SEQ_BLOCKS + i, 0)
    const2 = lambda b, i: (0, 0)
    const3 = lambda b, i: (0, 0, 0)
    per_seq = lambda b, i: (b, 0, 0)
    n_slab = D_CONV // LANES
    return pl.pallas_call(
        _mix_prompt_kernel,
        grid=(BATCH, SEQ_BLOCKS),
        in_specs=[
            pl.BlockSpec(memory_space=pltpu.SMEM),
            pl.BlockSpec((WINDOW, D_IN), cur),
            pl.BlockSpec((len(POOL_WINDOWS), POOL_GROUP, POOL_GROUP), const3),
            pl.BlockSpec((1, D_POOL), const2),
            pl.BlockSpec((CONV_WIDTH, D_CONV), const2),
            pl.BlockSpec((1, D_CONV), const2),
            pl.BlockSpec((1, D_CONV), const2),
            pl.BlockSpec((1, D_CONV), const2),
            pl.BlockSpec((D_CONV, D_CONV), const2),
            pl.BlockSpec((None, N_HEADS, WINDOW, 2 * WINDOW), lambda b, i: (jnp.minimum(i, 1), 0, 0, 0)),
        ],
        out_specs=[
            pl.BlockSpec((WINDOW, D_MODEL), cur),
            pl.BlockSpec((None, CONV_HALO, D_CONV), per_seq),
            pl.BlockSpec((None, POOL_HALO, D_POOL), per_seq),
            pl.BlockSpec((None, WINDOW, 2 * D_KV), per_seq),
        ],
        out_shape=[
            jax.ShapeDtypeStruct((M_PAD, D_MODEL), BF16),
            jax.ShapeDtypeStruct((BATCH, CONV_HALO, D_CONV), F32),
            jax.ShapeDtypeStruct((BATCH, POOL_HALO, D_POOL), F32),
            jax.ShapeDtypeStruct((BATCH, WINDOW, 2 * D_KV), F32),
        ],
        scratch_shapes=[
            pltpu.VMEM((n_slab, POOL_HALO + WINDOW, LANES), F32),
            pltpu.VMEM((n_slab, CONV_HALO + WINDOW, LANES), F32),
            pltpu.VMEM((WINDOW, D_CONV), F32),
            pltpu.VMEM((N_KV_HEADS, 2 * WINDOW, LANES), BF16),
            pltpu.VMEM((N_KV_HEADS, 2 * WINDOW, LANES), BF16),
        ],
        compiler_params=_params("arbitrary", "arbitrary"),
        name="mix_prompt",
    )(sink, z, wpool, pscale, wdw, bdw, lg, lb, wpw, bm2)


def _attn_sample_kernel(q_ref, kn_ref, vn_ref, ck_ref, cv_ref, bc_ref, bn_ref, sink_ref, o_ref):
    row_head = lax.broadcasted_iota(jnp.int32, (N_HEADS, D_KV), 0) // GQA_GROUP
    col_head = lax.broadcasted_iota(jnp.int32, (N_HEADS, D_KV), 1) // HEAD_DIM
    own = row_head == col_head
    sink = sink_ref[...]

    def body(b, carry):
        q = q_ref[b] * ATTN_SCALE
        qx = jnp.where(own, jnp.concatenate([q] * N_KV_HEADS, axis=1), 0.0)
        s_c = lax.dot_general(qx.astype(BF16), ck_ref[b].astype(BF16), (((1,), (1,)), ((), ())),
                              preferred_element_type=F32) + bc_ref[...]
        k_new = kn_ref[pl.ds(b, 1), :]
        s_n = jnp.sum(qx * k_new, axis=-1, keepdims=True) + bn_ref[...]
        m = jnp.maximum(jnp.maximum(jnp.max(s_c, axis=-1, keepdims=True), s_n), sink)
        e_c = jnp.exp(s_c - m)
        e_n = jnp.exp(s_n - m)
        denom = jnp.sum(e_c, axis=-1, keepdims=True) + e_n + jnp.exp(sink - m)
        o = jnp.dot(e_c.astype(BF16), cv_ref[b].astype(BF16), preferred_element_type=F32)
        o = (o + e_n * vn_ref[pl.ds(b, 1), :]) / denom
        o = jnp.where(own, o, 0.0)
        acc = o[:, 0:HEAD_DIM]
        for j in range(1, N_KV_HEADS):
            acc = acc + o[:, j * HEAD_DIM:(j + 1) * HEAD_DIM]
        o_ref[b] = acc
        return carry

    lax.fori_loop(0, DEC_BATCH, body, 0)


def _attn_sample(q3, k_new, v_new, ck, cv, l, bias_cache, bias_new, sink_col):
    full = lambda shape: pl.BlockSpec(shape, lambda i: (0,) * len(shape))
    layer = lambda i: (l, 0, 0, 0)
    return pl.pallas_call(
        _attn_sample_kernel,
        grid=(1,),
        in_specs=[
            full((DEC_BATCH, N_HEADS, HEAD_DIM)),
            full((DEC_BATCH, D_KV)),
            full((DEC_BATCH, D_KV)),
            pl.BlockSpec((None, DEC_BATCH, WINDOW, D_KV), layer),
            pl.BlockSpec((None, DEC_BATCH, WINDOW, D_KV), layer),
            full((N_HEADS, WINDOW)),
            full((N_HEADS, 1)),
            full((N_HEADS, 1)),
        ],
        out_specs=full((DEC_BATCH, N_HEADS, HEAD_DIM)),
        out_shape=jax.ShapeDtypeStruct((DEC_BATCH, N_HEADS, HEAD_DIM), F32),
        compiler_params=_params("arbitrary"),
        name="attn_sample",
    )(q3, k_new, v_new, ck, cv, bias_cache, bias_new, sink_col)


def _mix_sample_kernel(z_ref, at_ref, sp_ref, sc_ref, wpool_ref, pscale_ref, wdw_ref, bdw_ref,
                       lg_ref, lb_ref, wpw_ref, m_hbm_ref, o_ref, cnew_ref):
    del m_hbm_ref
    B = DEC_BATCH
    o_ref[...] = jnp.zeros(o_ref.shape, o_ref.dtype)

    u_new = z_ref[0:B, OFF_U:OFF_U + D_POOL]
    for g, w in enumerate(POOL_WINDOWS):
        sl = slice(g * POOL_GROUP, (g + 1) * POOL_GROUP)
        ws = u_new[:, sl]
        for k in range(1, w):
            ws = ws + sp_ref[POOL_STATE - k, :, sl]
        cnt = float(min(PAST_LEN + 1, w))
        d = ws / cnt - u_new[:, sl]
        y = jnp.dot(d.astype(BF16), wpool_ref[g], preferred_element_type=F32)
        o_ref[0:B, sl] = (y * pscale_ref[:, sl]).astype(BF16)

    c_new = z_ref[0:B, OFF_A:OFF_A + D_CONV] * jax.nn.sigmoid(z_ref[0:B, OFF_G:OFF_G + D_CONV])
    cnew_ref[...] = c_new
    acc = c_new * wdw_ref[CONV_STATE:CONV_STATE + 1, :]
    for j in range(CONV_STATE):
        acc = acc + sc_ref[j] * wdw_ref[j:j + 1, :]
    yc = _conv_tail(acc, bdw_ref[...], lg_ref[...], lb_ref[...], wpw_ref[...])
    o_ref[0:B, D_POOL:D_POOL + D_CONV] = yc.astype(BF16)

    o_ref[0:B, D_POOL + D_CONV:D_MODEL] = at_ref[...].astype(BF16)


def _mix_sample(z, attn, sp_t, sc_t, wpool, pscale, wdw, bdw, lg, lb, wpw, m_all):
    c2 = lambda i: (0, 0)
    c3 = lambda i: (0, 0, 0)
    return pl.pallas_call(
        _mix_sample_kernel,
        grid=(1,),
        in_specs=[
            pl.BlockSpec((WINDOW, D_IN), lambda i: (SAMPLE_BLOCK, 0)),
            pl.BlockSpec((DEC_BATCH, D_ATTN), c2),
            pl.BlockSpec((POOL_STATE, DEC_BATCH, D_POOL), c3),
            pl.BlockSpec((CONV_STATE, DEC_BATCH, D_CONV), c3),
            pl.BlockSpec((len(POOL_WINDOWS), POOL_GROUP, POOL_GROUP), c3),
            pl.BlockSpec((1, D_POOL), c2),
            pl.BlockSpec((CONV_WIDTH, D_CONV), c2),
            pl.BlockSpec((1, D_CONV), c2),
            pl.BlockSpec((1, D_CONV), c2),
            pl.BlockSpec((1, D_CONV), c2),
            pl.BlockSpec((D_CONV, D_CONV), c2),
            pl.BlockSpec(memory_space=pl.ANY),
        ],
        out_specs=[
            pl.BlockSpec((WINDOW, D_MODEL), lambda i: (SAMPLE_BLOCK, 0)),
            pl.BlockSpec((DEC_BATCH, D_CONV), c2),
        ],
        out_shape=[
            jax.ShapeDtypeStruct((M_PAD, D_MODEL), BF16),
            jax.ShapeDtypeStruct((DEC_BATCH, D_CONV), F32),
        ],
        input_output_aliases={11: 0},
        compiler_params=_params("arbitrary"),
        name="mix_sample",
    )(z, attn, sp_t, sc_t, wpool, pscale, wdw, bdw, lg, lb, wpw, m_all)


def _bucket_of_distance():
    n = np.arange(WINDOW)
    nf = np.maximum(n, 1).astype(np.float32)
    log_b = MAX_EXACT + (np.log(nf / MAX_EXACT) / math.log(MAX_DISTANCE / MAX_EXACT)
                         * (N_BUCKETS - MAX_EXACT)).astype(np.int32)
    return np.where(n < MAX_EXACT, n, np.minimum(log_b, N_BUCKETS - 1)).astype(np.int32)


def _bias_tables(rel_bias):
    bucket = _bucket_of_distance()
    assert np.all(np.diff(bucket) >= 0)
    counts = np.bincount(bucket, minlength=N_BUCKETS)
    rb = rel_bias.astype(F32)
    bias_t = jnp.concatenate([jnp.broadcast_to(rb[b:b + 1], (int(c), N_HEADS))
                              for b, c in enumerate(counts) if c > 0], axis=0).T
    rev = bias_t[:, ::-1]
    neg = lambda n: jnp.full((N_HEADS, n), NEG_INF, F32)
    period = 3 * WINDOW
    vec = jnp.concatenate([neg(1), rev, neg(period - 1 - WINDOW)], axis=1)
    bm = jnp.tile(vec, (1, WINDOW))[:, :WINDOW * (period - 1)].reshape(N_HEADS, WINDOW, period - 1)
    bm = bm[:, :, :2 * WINDOW]
    first = jnp.concatenate([jnp.full((N_HEADS, WINDOW, WINDOW), NEG_INF, F32), bm[:, :, WINDOW:]], axis=2)
    bm2 = jnp.stack([first, bm])
    bias_cache = jnp.concatenate([neg(1), rev[:, :WINDOW - 1]], axis=1)
    bias_new = bias_t[:, 0:1]
    return bm2, bias_cache, bias_new


def kernel(x_prompt, x_sample, p_prompt, p_sample, cache_k, cache_v, state_pool, state_conv, rel_bias, ln1_g, ln1_b, ffn1_w_gate, ffn1_w_up, ffn1_w_down, w_in, w_pool, pool_scale, w_dw, b_dw, conv_ln_g, conv_ln_b, w_pw, sinks, w_out, ln2_g, ln2_b, ffn2_w_gate, ffn2_w_up, ffn2_w_down, ln3_g, ln3_b, w_ple_gate, w_ple):
    pad_rows = M_PAD - M_PROMPT - DEC_BATCH
    hf = jnp.concatenate([x_prompt.reshape(M_PROMPT, D_MODEL), x_sample.reshape(DEC_BATCH, D_MODEL),
                          jnp.zeros((pad_rows, D_MODEL), F32)], axis=0)
    hb = hf.astype(BF16)
    p_all = jnp.concatenate([p_prompt.reshape(DEPTH, M_PROMPT, D_PLE),
                             p_sample.reshape(DEPTH, DEC_BATCH, D_PLE),
                             jnp.zeros((DEPTH, pad_rows, D_PLE), F32)], axis=1).astype(BF16)

    bm2, bias_cache, bias_new = _bias_tables(rel_bias)
    sinks = sinks.astype(F32)
    sink_col = sinks[..., None]

    ck = cache_k.reshape(DEPTH, DEC_BATCH, WINDOW, D_KV)
    cv = cache_v.reshape(DEPTH, DEC_BATCH, WINDOW, D_KV)
    sp_t = jnp.transpose(state_pool, (0, 2, 1, 3))
    sc_t = jnp.transpose(state_conv, (0, 2, 1, 3))

    row = lambda v, l: v[l][None, :].astype(F32)
    heads = lambda t: t.reshape(t.shape[0], t.shape[1], N_KV_HEADS, HEAD_DIM)
    nk_p, nv_p, npool_p, nconv_p = [], [], [], []
    nk_s, nv_s, npool_s, nconv_s = [], [], [], []

    for l in range(DEPTH):
        a, wd_b = _ffn_up(hb, ffn1_w_gate, ffn1_w_up, ffn1_w_down, l)
        hf, hb = _proj_ln(a, wd_b, hf, row(ln1_g, l), row(ln1_b, l), scale=0.5, tm=320)

        z, wo_b = _w_in(hb, w_in, w_out, l)
        mixer_w = (w_pool[l].astype(BF16), row(pool_scale, l), w_dw[l].astype(F32), row(b_dw, l),
                   row(conv_ln_g, l), row(conv_ln_b, l), w_pw[l].astype(BF16))
        m_all, ctail, utail, kvtail = _mix_prompt(z, sinks[l], *mixer_w, bm2)

        zs = z[M_PROMPT:M_PROMPT + DEC_BATCH]
        k_new = zs[:, OFF_K:OFF_K + D_KV]
        v_new = zs[:, OFF_V:OFF_V + D_KV]
        q3 = zs[:, OFF_Q:OFF_Q + D_ATTN].reshape(DEC_BATCH, N_HEADS, HEAD_DIM)
        attn_s = _attn_sample(q3, k_new, v_new, ck, cv, l, bias_cache, bias_new, sink_col[l])
        m_all, c_new = _mix_sample(z, attn_s.reshape(DEC_BATCH, D_ATTN), sp_t[l], sc_t[l],
                                   *mixer_w, m_all)

        hf, hb = _proj_ln(m_all, wo_b, hf, row(ln2_g, l), row(ln2_b, l), scale=1.0, tm=640)

        a, wd_b = _ffn_up(hb, ffn2_w_gate, ffn2_w_up, ffn2_w_down, l)
        hf, hb = _proj_ln(a, wd_b, hf, row(ln3_g, l), row(ln3_b, l), scale=0.5, tm=320)
        hf, hb = _ple(hb, hf, p_all, w_ple_gate, w_ple, l)

        nk_p.append(heads(kvtail[:, :, :D_KV]))
        nv_p.append(heads(kvtail[:, :, D_KV:]))
        npool_p.append(utail[:, POOL_HALO - POOL_STATE:, :])
        nconv_p.append(ctail[:, CONV_HALO - CONV_STATE:, :])
        nk_s.append(jnp.concatenate([cache_k[l][:, 1:], heads(k_new[:, None, :])], axis=1))
        nv_s.append(jnp.concatenate([cache_v[l][:, 1:], heads(v_new[:, None, :])], axis=1))
        npool_s.append(jnp.concatenate(
            [state_pool[l][:, 1:], zs[:, None, OFF_U:OFF_U + D_POOL]], axis=1))
        nconv_s.append(jnp.concatenate([state_conv[l][:, 1:], c_new[:, None, :]], axis=1))

    y_prompt = hf[:M_PROMPT].reshape(BATCH, SEQ, D_MODEL)
    y_sample = hf[M_PROMPT:M_PROMPT + DEC_BATCH].reshape(DEC_BATCH, 1, D_MODEL)
    st = lambda xs: jnp.stack(xs)
    return (y_prompt, y_sample, st(nk_p), st(nv_p), st(npool_p), st(nconv_p),
            st(nk_s), st(nv_s), st(npool_s), st(nconv_s))
```
